```python
import math, functools
import jax, jax.numpy as jnp
from jax import lax
import numpy as np

D_MODEL = 1024
BATCH = 4
SEQ = 4096
DEPTH = 4
DEC_BATCH = 128
DEC_SEQ = 1
PAST_LEN = 2048
PAGE_SIZE = 128

N_EVEN = (DEPTH + 1) // 2
N_ODD = DEPTH // 2
H_A = 4
DK_A = 64
DV_A = 2 * DK_A
Q_BLOCK = 128
H_B = 4
DK_B = 64
DV_B = 128
MLSTM_CHUNK = 128
GATE_CAP = 15.0
D_RNN = 1280
N_BLK = 5
BW = D_RNN // N_BLK
CONV_W = 4
LRU_C = 8.0
D_FF = -(-8 * D_MODEL // (3 * 256)) * 256
ALPHA = (2.0 * DEPTH) ** 0.25
BETA = (8.0 * DEPTH) ** -0.25
ALIBI_SLOPES = np.array([2.0 ** (-8.0 * (h + 1) / H_A) for h in range(H_A)], np.float32)
EVEN_SIZES = [H_A * 2 * DK_A, H_A * 2 * DK_A, H_A * DV_A, H_B * DK_B, H_B * DK_B, H_B * DV_B, H_B * DV_B, H_B, H_B]
EVEN_SPLITS = np.cumsum(EVEN_SIZES)[:-1].tolist()
P_EVEN = sum(EVEN_SIZES)
D_MIX_EVEN = H_A * DV_A + H_B * DV_B

kernel_name = "hybrid_diffattn_mlstm_rglru_decoder_step"


def layer_norm(x, g, b, eps=1e-5):
    xf = x.astype(jnp.float32)
    mu = xf.mean(-1, keepdims=True)
    var = jnp.square(xf - mu).mean(-1, keepdims=True)
    return ((xf - mu) * lax.rsqrt(var + eps) * g + b).astype(x.dtype)


def rms_norm(x, g, eps=1e-5):
    xf = x.astype(jnp.float32)
    return (xf * lax.rsqrt(jnp.mean(xf * xf, -1, keepdims=True) + eps) * g).astype(x.dtype)


def diff_attn_prompt(q, k, v, lam):
    bsz, t_len = q.shape[:2]
    n_qb = t_len // Q_BLOCK
    qf = q.astype(jnp.float32) * (DK_A ** -0.5)
    kf = k.astype(jnp.float32)
    vf = v.astype(jnp.float32)
    q_blocks = qf.reshape(bsz, n_qb, Q_BLOCK, H_A, 2, DK_A).transpose(1, 0, 2, 3, 4, 5)
    k_pos = jnp.arange(t_len)
    slopes = jnp.asarray(ALIBI_SLOPES)[:, None, None, None]

    def one_block(args):
        q_blk, blk = args
        q_pos = blk * Q_BLOCK + jnp.arange(Q_BLOCK)
        dist = q_pos[:, None] - k_pos[None, :]
        s = jnp.einsum('bqhjd,bshjd->bhjqs', q_blk, kf)
        s = s - slopes * dist.astype(jnp.float32)
        s = jnp.where(dist >= 0, s, -jnp.inf)
        p = jax.nn.softmax(s, axis=-1)
        w = p[:, :, 0] - lam * p[:, :, 1]
        return jnp.einsum('bhqs,bshv->bqhv', w, vf)

    out = lax.map(one_block, (q_blocks, jnp.arange(n_qb)))
    return out.transpose(1, 0, 2, 3, 4).reshape(bsz, t_len, H_A, DV_A)


def diff_attn_sample(q, k_new, v_new, lam, k_past, v_past):
    past = k_past.shape[1]
    t_len = q.shape[1]
    qf = q.astype(jnp.float32) * (DK_A ** -0.5)
    slopes = jnp.asarray(ALIBI_SLOPES)[:, None, None, None]
    q_pos = past + jnp.arange(t_len)
    dist_past = (q_pos[:, None] - jnp.arange(past)[None, :]).astype(jnp.float32)
    dist_new = q_pos[:, None] - q_pos[None, :]
    s_past = jnp.einsum('bthjd,bshjd->bhjts', qf, k_past.astype(jnp.float32)) - slopes * dist_past
    s_new = jnp.einsum('bthjd,bshjd->bhjts', qf, k_new.astype(jnp.float32)) - slopes * dist_new.astype(jnp.float32)
    s_new = jnp.where(dist_new >= 0, s_new, -jnp.inf)
    p = jax.nn.softmax(jnp.concatenate([s_past, s_new], axis=-1), axis=-1)
    w = p[:, :, 0] - lam * p[:, :, 1]
    return (jnp.einsum('bhts,bshv->bthv', w[..., :past], v_past.astype(jnp.float32))
            + jnp.einsum('bhts,bshv->bthv', w[..., past:], v_new.astype(jnp.float32)))


def mlstm_chunk(carry, xs):
    c_prev, n_prev, m_prev = carry
    q, k, v, ig, lf = xs
    length = q.shape[2]
    b = jnp.cumsum(lf, axis=-1)
    causal = jnp.tril(jnp.ones((length, length), bool))
    d = jnp.where(causal, b[..., :, None] - b[..., None, :] + ig[..., None, :], -jnp.inf)
    inter = b + m_prev[..., None]
    m_t = jnp.maximum(inter, d.max(-1))
    w = jnp.exp(d - m_t[..., None])
    g_inter = jnp.exp(inter - m_t)
    s = jnp.einsum('bhtk,bhsk->bhts', q, k) * w
    num = jnp.einsum('bhts,bhsv->bhtv', s, v) + g_inter[..., None] * jnp.einsum('bhtk,bhvk->bhtv', q, c_prev)
    den = s.sum(-1) + g_inter * jnp.einsum('bhtk,bhk->bht', q, n_prev)
    h = num / jnp.maximum(jnp.abs(den), jnp.exp(-m_t))[..., None]
    g_tot = b[..., -1]
    dl = g_tot[..., None] - b + ig
    m_new = jnp.maximum(g_tot + m_prev, dl.max(-1))
    wl = jnp.exp(dl - m_new[..., None])
    decay = jnp.exp(g_tot + m_prev - m_new)
    c_new = decay[..., None, None] * c_prev + jnp.einsum('bhs,bhsv,bhsk->bhvk', wl, v, k)
    n_new = decay[..., None] * n_prev + jnp.einsum('bhs,bhsk->bhk', wl, k)
    return (c_new, n_new, m_new), h


def mlstm_sequence(q, k, v, ig, lf, c0, n0, m0, chunk):
    bsz, t_len = q.shape[:2]
    n_chunks = t_len // chunk

    def to_chunks(t):
        t = t.reshape((bsz, n_chunks, chunk) + t.shape[2:])
        return jnp.swapaxes(jnp.moveaxis(t, 1, 0), 2, 3)

    init = (c0.astype(jnp.float32), n0.astype(jnp.float32), m0.astype(jnp.float32))
    xs = (to_chunks(q), to_chunks(k), to_chunks(v), to_chunks(ig), to_chunks(lf))
    final, hs = lax.scan(mlstm_chunk, init, xs)
    hs = jnp.moveaxis(jnp.swapaxes(hs, 2, 3), 0, 1).reshape(bsz, t_len, H_B, DV_B)
    return hs, final


def causal_conv(x_ext, w, b):
    t_len = x_ext.shape[1] - (CONV_W - 1)
    y = b
    for j in range(CONV_W):
        y = y + w[j] * x_ext[:, j:j + t_len]
    return y


def linear_scan(a, b, h0):
    def combine(left, right):
        return left[0] * right[0], right[0] * left[1] + right[1]
    a_cum, b_cum = lax.associative_scan(combine, (a, b), axis=1)
    return a_cum * h0[:, None] + b_cum


def setup_inputs(seed: int = 0) -> dict:
    key = jax.random.key(seed)
    ks = iter(jax.random.split(key, 48))

    def nrm(shape, scale=1.0):
        return scale * jax.random.normal(next(ks), shape, jnp.float32)

    n_pages = PAST_LEN // PAGE_SIZE
    n_used = DEC_BATCH * n_pages
    n_pool = n_used + n_used // 4
    perm = jax.random.permutation(next(ks), n_pool)
    page_table = perm[:n_used].reshape(DEC_BATCH, n_pages).astype(jnp.int32)

    u_lru = jax.random.uniform(next(ks), (N_ODD, D_RNN), jnp.float32, 0.9, 0.999)
    a0 = u_lru ** (1.0 / LRU_C)
    lru_lambda = jnp.log(a0) - jnp.log1p(-a0)

    return {
        "x_prompt": nrm((BATCH, SEQ, D_MODEL)),
        "x_sample": nrm((DEC_BATCH, DEC_SEQ, D_MODEL)),
        "cache_k": nrm((N_EVEN, n_pool, PAGE_SIZE, H_A, 2, DK_A)),
        "cache_v": nrm((N_EVEN, n_pool, PAGE_SIZE, H_A, DV_A)),
        "state_C": nrm((N_EVEN, DEC_BATCH, H_B, DV_B, DK_B)),
        "state_n": nrm((N_EVEN, DEC_BATCH, H_B, DK_B)),
        "state_m": nrm((N_EVEN, DEC_BATCH, H_B)),
        "state_h": nrm((N_ODD, DEC_BATCH, D_RNN), 0.5),
        "state_conv": nrm((N_ODD, DEC_BATCH, CONV_W - 1, D_RNN)),
        "page_table": page_table,
        "c_prompt": nrm((BATCH, D_MODEL)),
        "c_sample": nrm((DEC_BATCH, D_MODEL)),
        "w_ada": nrm((DEPTH, D_MODEL, 6 * D_MODEL), 0.5 * D_MODEL ** -0.5),
        "b_ada": nrm((DEPTH, 6 * D_MODEL), 0.01),
        "ln_g": 1.0 + nrm((DEPTH, 2, D_MODEL), 0.01),
        "ln_b": nrm((DEPTH, 2, D_MODEL), 0.01),
        "w_in_even": nrm((N_EVEN, D_MODEL, P_EVEN), D_MODEL ** -0.5),
        "w_out_even": nrm((N_EVEN, D_MIX_EVEN, D_MODEL), BETA * D_MIX_EVEN ** -0.5),
        "lam": nrm((N_EVEN, 4, DK_A), 0.1),
        "attn_norm_g": 1.0 + nrm((N_EVEN, DV_A), 0.01),
        "mlstm_norm_g": 1.0 + nrm((N_EVEN, H_B, DV_B), 0.01),
        "b_i": nrm((N_EVEN, H_B), 0.1),
        "b_f": jax.random.uniform(next(ks), (N_EVEN, H_B), jnp.float32, 3.0, 6.0),
        "w_in_odd": nrm((N_ODD, D_MODEL, 2 * D_RNN), D_MODEL ** -0.5),
        "w_out_odd": nrm((N_ODD, D_RNN, D_MODEL), BETA * D_RNN ** -0.5),
        "conv_w": nrm((N_ODD, CONV_W, D_RNN), CONV_W ** -0.5),
        "conv_b": nrm((N_ODD, D_RNN), 0.01),
        "w_a": nrm((N_ODD, N_BLK, BW, BW), BW ** -0.5),
        "b_a": nrm((N_ODD, D_RNN), 0.01),
        "w_x": nrm((N_ODD, N_BLK, BW, BW), BW ** -0.5),
        "b_x": nrm((N_ODD, D_RNN), 0.01),
        "lru_lambda": lru_lambda,
        "w_ffn_in": nrm((DEPTH, D_MODEL, 2 * D_FF), D_MODEL ** -0.5),
        "w_ffn_out": nrm((DEPTH, D_FF, D_MODEL), BETA * D_FF ** -0.5),
    }


def reference(x_prompt, x_sample, cache_k, cache_v, state_C, state_n, state_m, state_h, state_conv, page_table,
              c_prompt, c_sample, w_ada, b_ada, ln_g, ln_b, w_in_even, w_out_even, lam, attn_norm_g, mlstm_norm_g,
              b_i, b_f, w_in_odd, w_out_odd, conv_w, conv_b, w_a, b_a, w_x, b_x, lru_lambda, w_ffn_in, w_ffn_out):
    f32 = jnp.float32

    def modulations(c, li):
        mod = jax.nn.silu(c) @ w_ada[li] + b_ada[li]
        return jnp.split(mod[:, None, :], 6, axis=-1)

    def post_norm(x, sub, gate, li, j):
        return layer_norm(ALPHA * x + (1.0 + gate) * sub, ln_g[li, j], ln_b[li, j])

    def even_mixer(u, li, attend, c0, n0, m0, chunk):
        e = li // 2
        bsz, t_len, _ = u.shape
        aq, ak, av, bq, bk, bv, bo, bi, bf = jnp.split(u @ w_in_even[e], EVEN_SPLITS, axis=-1)
        aq = aq.reshape(bsz, t_len, H_A, 2, DK_A)
        ak = ak.reshape(bsz, t_len, H_A, 2, DK_A)
        av = av.reshape(bsz, t_len, H_A, DV_A)
        lam_init = 0.8 - 0.6 * math.exp(-0.3 * li)
        lp = lam[e].astype(f32)
        lam_val = jnp.exp(jnp.sum(lp[0] * lp[1])) - jnp.exp(jnp.sum(lp[2] * lp[3])) + lam_init
        a_out = attend(aq, ak, av, lam_val)
        a_out = rms_norm(a_out, attn_norm_g[e]) * (1.0 - lam_init)
        q = bq.reshape(bsz, t_len, H_B, DK_B).astype(f32) * (DK_B ** -0.5)
        k = bk.reshape(bsz, t_len, H_B, DK_B).astype(f32)
        v = bv.reshape(bsz, t_len, H_B, DV_B).astype(f32)
        ig = GATE_CAP * jnp.tanh((bi + b_i[e]).astype(f32) / GATE_CAP)
        lf = jax.nn.log_sigmoid((bf + b_f[e]).astype(f32))
        h, (c_n, n_n, m_n) = mlstm_sequence(q, k, v, ig, lf, c0, n0, m0, chunk)
        h = rms_norm(h, mlstm_norm_g[e]) * jax.nn.sigmoid(bo.astype(f32)).reshape(bsz, t_len, H_B, DV_B)
        mixed = jnp.concatenate([a_out.reshape(bsz, t_len, -1), h.reshape(bsz, t_len, -1)], axis=-1).astype(u.dtype)
        return mixed @ w_out_even[e], ak, av, c_n, n_n, m_n

    def odd_mixer(u, li, h0, conv0):
        o = li // 2
        bsz, t_len, _ = u.shape
        g, xr = jnp.split(u @ w_in_odd[o], 2, axis=-1)
        x_ext = jnp.concatenate([conv0.astype(xr.dtype), xr], axis=1)
        xc = causal_conv(x_ext, conv_w[o], conv_b[o]).astype(f32)
        xb = xc.reshape(bsz, t_len, N_BLK, BW)
        r = jax.nn.sigmoid(jnp.einsum('btnc,ncd->btnd', xb, w_a[o].astype(f32)).reshape(bsz, t_len, D_RNN) + b_a[o])
        i = jax.nn.sigmoid(jnp.einsum('btnc,ncd->btnd', xb, w_x[o].astype(f32)).reshape(bsz, t_len, D_RNN) + b_x[o])
        log_a = -LRU_C * r * jax.nn.softplus(-lru_lambda[o].astype(f32))
        a = jnp.exp(log_a)
        b = jnp.sqrt(-jnp.expm1(2.0 * log_a)) * (i * xc)
        h = linear_scan(a, b, h0.astype(f32))
        y = (h.astype(u.dtype) * jax.nn.gelu(g)) @ w_out_odd[o]
        return y, h[:, -1], x_ext[:, -(CONV_W - 1):]

    def ffn(u, li):
        gate, up = jnp.split(u @ w_ffn_in[li], 2, axis=-1)
        return (jax.nn.silu(gate) * up) @ w_ffn_out[li]

    xp, xs = x_prompt, x_sample
    bp, bs = xp.shape[0], xs.shape[0]
    k_p, v_p, k_s, v_s = [], [], [], []
    cp_l, np_l, mp_l, cs_l, ns_l, ms_l = [], [], [], [], [], []
    hp_l, convp_l, hs_l, convs_l = [], [], [], []

    for li in range(DEPTH):
        mod_p = modulations(c_prompt, li)
        mod_s = modulations(c_sample, li)
        up = xp * (1.0 + mod_p[1]) + mod_p[0]
        us = xs * (1.0 + mod_s[1]) + mod_s[0]
        if li % 2 == 0:
            e = li // 2
            yp, akp, avp, c_n, n_n, m_n = even_mixer(
                up, li, diff_attn_prompt,
                jnp.zeros((bp, H_B, DV_B, DK_B), f32), jnp.zeros((bp, H_B, DK_B), f32), jnp.zeros((bp, H_B), f32),
                MLSTM_CHUNK)
            k_p.append(akp.reshape(bp, -1, PAGE_SIZE, H_A, 2, DK_A))
            v_p.append(avp.reshape(bp, -1, PAGE_SIZE, H_A, DV_A))
            cp_l.append(c_n); np_l.append(n_n); mp_l.append(m_n)
            k_past = cache_k[e, page_table].reshape(bs, -1, H_A, 2, DK_A)
            v_past = cache_v[e, page_table].reshape(bs, -1, H_A, DV_A)
            attend_s = functools.partial(diff_attn_sample, k_past=k_past, v_past=v_past)
            ys, aks, avs, c_n, n_n, m_n = even_mixer(us, li, attend_s, state_C[e], state_n[e], state_m[e], xs.shape[1])
            k_s.append(aks); v_s.append(avs)
            cs_l.append(c_n); ns_l.append(n_n); ms_l.append(m_n)
        else:
            o = li // 2
            yp, h_n, conv_n = odd_mixer(up, li, jnp.zeros((bp, D_RNN), f32), jnp.zeros((bp, CONV_W - 1, D_RNN), xp.dtype))
            hp_l.append(h_n); convp_l.append(conv_n)
            ys, h_n, conv_n = odd_mixer(us, li, state_h[o], state_conv[o])
            hs_l.append(h_n); convs_l.append(conv_n)
        xp = post_norm(xp, yp, mod_p[2], li, 0)
        xs = post_norm(xs, ys, mod_s[2], li, 0)
        up = xp * (1.0 + mod_p[4]) + mod_p[3]
        us = xs * (1.0 + mod_s[4]) + mod_s[3]
        xp = post_norm(xp, ffn(up, li), mod_p[5], li, 1)
        xs = post_norm(xs, ffn(us, li), mod_s[5], li, 1)

    k_prompt = jnp.stack(k_p)
    v_prompt = jnp.stack(v_p)
    k_sample = jnp.stack(k_s)
    v_sample = jnp.stack(v_s)
    C_prompt, n_prompt, m_prompt = jnp.stack(cp_l), jnp.stack(np_l), jnp.stack(mp_l)
    C_sample, n_sample, m_sample = jnp.stack(cs_l), jnp.stack(ns_l), jnp.stack(ms_l)
    h_prompt, conv_prompt = jnp.stack(hp_l), jnp.stack(convp_l)
    h_sample, conv_sample = jnp.stack(hs_l), jnp.stack(convs_l)
    return (xp, xs, k_prompt, v_prompt, k_sample, v_sample, C_prompt, n_prompt, m_prompt,
            C_sample, n_sample, m_sample, h_prompt, conv_prompt, h_sample, conv_sample)
```

```python
import functools
import math

import jax
import jax.numpy as jnp
import numpy as np
from jax import lax
from jax.experimental import pallas as pl
from jax.experimental.pallas import tpu as pltpu

F32 = jnp.float32
BF16 = jnp.bfloat16

GATE_CAP = 15.0
LRU_C = 8.0
LN_EPS = 1e-5
NEG_BIG = -1e30

LANES = 128
SUBLANES = 8
VMEM_LIMIT_BYTES = 56 * 1024 * 1024


def _cparams(*sem):
    return pltpu.CompilerParams(dimension_semantics=sem, vmem_limit_bytes=VMEM_LIMIT_BYTES)


def _dot(a, b):
    return jnp.dot(a, b, preferred_element_type=F32)


def _dot_nt(a, b):
    return lax.dot_general(a, b, (((1,), (1,)), ((), ())), preferred_element_type=F32)


def _dot_tn(a, b):
    return lax.dot_general(a, b, (((0,), (0,)), ((), ())), preferred_element_type=F32)


def _split3(x):
    hi = x.astype(BF16)
    r1 = x - hi.astype(F32)
    mid = r1.astype(BF16)
    lo = (r1 - mid.astype(F32)).astype(BF16)
    return hi, mid, lo


def _log_sigmoid(x):
    return jnp.minimum(x, 0.0) - jnp.log1p(jnp.exp(-jnp.abs(x)))


def _softplus(x):
    return jnp.maximum(x, 0.0) + jnp.log1p(jnp.exp(-jnp.abs(x)))


def _sigmoid(x):
    return 1.0 / (1.0 + jnp.exp(-x))


def _layer_norm_rows(z, g, b):
    mu = jnp.mean(z, axis=-1, keepdims=True)
    zc = z - mu
    var = jnp.mean(zc * zc, axis=-1, keepdims=True)
    return zc * lax.rsqrt(var + LN_EPS) * g + b


def _mod_body(c_ref, w_ref, b_ref, o_ref):
    c = c_ref[...]
    a = (c * _sigmoid(c)).astype(BF16)
    o_ref[...] = _dot(a, w_ref[...]) + b_ref[...]


def _modulations(c_all, w_ada, b_ada):
    depth, d, n = w_ada.shape
    r = c_all.shape[0]
    tn = 1536
    assert n % tn == 0
    return pl.pallas_call(
        _mod_body,
        grid=(depth, n // tn),
        in_specs=[
            pl.BlockSpec((r, d), lambda l, j: (0, 0)),
            pl.BlockSpec((None, d, tn), lambda l, j: (l, 0, j)),
            pl.BlockSpec((None, 1, tn), lambda l, j: (l, 0, j)),
        ],
        out_specs=pl.BlockSpec((None, r, tn), lambda l, j: (l, 0, j)),
        out_shape=jax.ShapeDtypeStruct((depth, r, n), F32),
        compiler_params=_cparams("parallel", "parallel"),
        name="modulations",
    )(c_all, w_ada, b_ada.reshape(depth, 1, n))


def _inproj_body(x_ref, sh_ref, sc_ref, *refs, plan):
    n_w = len(plan)
    w_refs = refs[:n_w]
    o_refs = refs[n_w:]
    u = (x_ref[...] * (1.0 + sc_ref[...]) + sh_ref[...]).astype(BF16)
    oi = 0
    for w_ref, outs in zip(w_refs, plan):
        y = _dot(u, w_ref[...])
        for scale, dtype in outs:
            o_refs[oi][...] = (y if scale == 1.0 else y * scale).astype(dtype)
            oi += 1


def _inproj(x, mod, li, k_shift, weights, plan, tm, name):
    bsz, t, d = x.shape
    r = mod.shape[2]
    assert t % tm == 0 and (r == 1 or r == tm == t)

    def mod_spec(k):
        return pl.BlockSpec((None, None, r, d), lambda b, i: (li, b, 0, k))

    in_specs = [pl.BlockSpec((None, tm, d), lambda b, i: (b, i, 0)), mod_spec(k_shift), mod_spec(k_shift + 1)]
    out_specs, out_shape = [], []
    for w, outs in zip(weights, plan):
        n = w.shape[1]
        in_specs.append(pl.BlockSpec((d, n), lambda b, i: (0, 0)))
        for _, dtype in outs:
            out_specs.append(pl.BlockSpec((None, tm, n), lambda b, i: (b, i, 0)))
            out_shape.append(jax.ShapeDtypeStruct((bsz, t, n), dtype))
    return pl.pallas_call(
        functools.partial(_inproj_body, plan=plan),
        grid=(bsz, t // tm),
        in_specs=in_specs,
        out_specs=out_specs,
        out_shape=out_shape,
        compiler_params=_cparams("parallel", "parallel"),
        name=name,
    )(x, mod, mod, *weights)


def _outproj_body(*refs, n_in, alpha):
    a_refs = refs[:n_in]
    w_refs = refs[n_in:2 * n_in]
    x_ref, gate_ref, g_ref, b_ref, o_ref = refs[2 * n_in:]
    y = _dot(a_refs[0][...], w_refs[0][...])
    for a_ref, w_ref in zip(a_refs[1:], w_refs[1:]):
        y = y + _dot(a_ref[...], w_ref[...])
    z = alpha * x_ref[...] + (1.0 + gate_ref[...]) * y
    o_ref[...] = _layer_norm_rows(z, g_ref[...], b_ref[...])


def _outproj_ln(acts, weights, x, mod, li, k_gate, ln_g, ln_b, alpha, tm, name):
    bsz, t, d = x.shape
    r = mod.shape[2]
    n_in = len(acts)
    in_specs = []
    for a in acts:
        in_specs.append(pl.BlockSpec((None, tm, a.shape[2]), lambda b, i: (b, i, 0)))
    for w in weights:
        in_specs.append(pl.BlockSpec(w.shape, lambda b, i: (0, 0)))
    in_specs += [
        pl.BlockSpec((None, tm, d), lambda b, i: (b, i, 0)),
        pl.BlockSpec((None, None, r, d), lambda b, i: (li, b, 0, k_gate)),
        pl.BlockSpec((1, d), lambda b, i: (0, 0)),
        pl.BlockSpec((1, d), lambda b, i: (0, 0)),
    ]
    return pl.pallas_call(
        functools.partial(_outproj_body, n_in=n_in, alpha=alpha),
        grid=(bsz, t // tm),
        in_specs=in_specs,
        out_specs=pl.BlockSpec((None, tm, d), lambda b, i: (b, i, 0)),
        out_shape=jax.ShapeDtypeStruct((bsz, t, d), F32),
        compiler_params=_cparams("parallel", "parallel"),
        name=name,
    )(*acts, *weights, x, mod, ln_g.reshape(1, d), ln_b.reshape(1, d))


def _ffn_body(x_ref, sh_ref, sc_ref, gate_ref, wg_ref, wu_ref, wo_ref, g_ref, b_ref, o_ref, u_sc, acc_sc, *, alpha):
    f = pl.program_id(2)

    @pl.when(f == 0)
    def _():
        u_sc[...] = (x_ref[...] * (1.0 + sc_ref[...]) + sh_ref[...]).astype(BF16)

    u = u_sc[...]
    gt = _dot(u, wg_ref[...])
    up = _dot(u, wu_ref[...])
    h = (gt * _sigmoid(gt) * up).astype(BF16)
    part = _dot(h, wo_ref[...])

    @pl.when(f == 0)
    def _():
        acc_sc[...] = part

    @pl.when(f != 0)
    def _():
        acc_sc[...] += part

    @pl.when(f == pl.num_programs(2) - 1)
    def _():
        z = alpha * x_ref[...] + (1.0 + gate_ref[...]) * acc_sc[...]
        o_ref[...] = _layer_norm_rows(z, g_ref[...], b_ref[...])


def _ffn_ln(x, mod, li, w_gate, w_up, w_out, ln_g, ln_b, alpha, tm, tf, name):
    bsz, t, d = x.shape
    r = mod.shape[2]
    d_ff = w_gate.shape[1]
    assert d_ff % tf == 0 and t % tm == 0

    def mod_spec(k):
        return pl.BlockSpec((None, None, r, d), lambda b, i, f: (li, b, 0, k))

    return pl.pallas_call(
        functools.partial(_ffn_body, alpha=alpha),
        grid=(bsz, t // tm, d_ff // tf),
        in_specs=[
            pl.BlockSpec((None, tm, d), lambda b, i, f: (b, i, 0)),
            mod_spec(3), mod_spec(4), mod_spec(5),
            pl.BlockSpec((d, tf), lambda b, i, f: (0, f)),
            pl.BlockSpec((d, tf), lambda b, i, f: (0, f)),
            pl.BlockSpec((tf, d), lambda b, i, f: (f, 0)),
            pl.BlockSpec((1, d), lambda b, i, f: (0, 0)),
            pl.BlockSpec((1, d), lambda b, i, f: (0, 0)),
        ],
        out_specs=pl.BlockSpec((None, tm, d), lambda b, i, f: (b, i, 0)),
        out_shape=jax.ShapeDtypeStruct((bsz, t, d), F32),
        scratch_shapes=[pltpu.VMEM((tm, d), BF16), pltpu.VMEM((tm, d), F32)],
        compiler_params=_cparams("parallel", "parallel", "arbitrary"),
        name=name,
    )(x, mod, mod, mod, w_gate, w_up, w_out, ln_g.reshape(1, d), ln_b.reshape(1, d))


def _lambda_value(lam_ref, lam_init):
    lp = lam_ref[...]
    s01 = jnp.sum(lp[0:1, :] * lp[1:2, :], axis=-1, keepdims=True)
    s23 = jnp.sum(lp[2:3, :] * lp[3:4, :], axis=-1, keepdims=True)
    return jnp.exp(s01) - jnp.exp(s23) + lam_init


def _attn_body(qi_ref, ki_ref, slope_ref, lam_ref, g_ref, q_ref, k_ref, v_ref, o_ref, m_sc, l_sc, acc_sc,
               *, tile, dk, lam_init, out_scale):
    h = pl.program_id(1)
    p = pl.program_id(2)
    qi = qi_ref[p]
    ki = ki_ref[p]

    @pl.when(ki == 0)
    def _():
        m_sc[...] = jnp.full(m_sc.shape, NEG_BIG, F32)
        l_sc[...] = jnp.zeros(l_sc.shape, F32)
        acc_sc[...] = jnp.zeros(acc_sc.shape, F32)

    q = q_ref[...]
    k = k_ref[...]
    v = v_ref[...]
    lane = lax.broadcasted_iota(jnp.int32, q.shape, 1)
    rows = lax.broadcasted_iota(jnp.int32, (tile, tile), 0)
    cols = lax.broadcasted_iota(jnp.int32, (tile, tile), 1)
    dist = (qi - ki) * tile + rows - cols
    bias = jnp.where(dist >= 0, -slope_ref[h] * dist.astype(F32), -jnp.inf)
    for j in range(2):
        qj = jnp.where((lane >= j * dk) & (lane < (j + 1) * dk), q, jnp.zeros_like(q))
        s = _dot_nt(qj, k) + bias
        m_prev = m_sc[j]
        m_new = jnp.maximum(m_prev, jnp.max(s, axis=-1, keepdims=True))
        alpha = jnp.exp(m_prev - m_new)
        pj = jnp.exp(s - m_new)
        l_sc[j] = alpha * l_sc[j] + jnp.sum(pj, axis=-1, keepdims=True)
        acc_sc[j] = alpha * acc_sc[j] + _dot(pj.astype(BF16), v)
        m_sc[j] = m_new

    @pl.when(ki == qi)
    def _():
        lam = _lambda_value(lam_ref, lam_init)
        out = acc_sc[0] / l_sc[0] - lam * (acc_sc[1] / l_sc[1])
        ms = jnp.mean(out * out, axis=-1, keepdims=True)
        o_ref[...] = (out * lax.rsqrt(ms + LN_EPS) * g_ref[...] * out_scale).astype(o_ref.dtype)


def _attn_prompt(q, k, v, slopes, lam_e, g, lam_init, n_heads, tile):
    bsz, t, _ = q.shape
    dv = g.shape[-1]
    assert t % tile == 0
    nt = t // tile
    pairs = [(a, b) for a in range(nt) for b in range(a + 1)]
    qi_tab = jnp.asarray(np.array([a for a, _ in pairs], np.int32))
    ki_tab = jnp.asarray(np.array([b for _, b in pairs], np.int32))
    grid_spec = pltpu.PrefetchScalarGridSpec(
        num_scalar_prefetch=2,
        grid=(bsz, n_heads, len(pairs)),
        in_specs=[
            pl.BlockSpec(memory_space=pltpu.SMEM),
            pl.BlockSpec((4, lam_e.shape[1]), lambda b, h, p, qt, kt: (0, 0)),
            pl.BlockSpec((1, dv), lambda b, h, p, qt, kt: (0, 0)),
            pl.BlockSpec((None, tile, dv), lambda b, h, p, qt, kt: (b, qt[p], h)),
            pl.BlockSpec((None, tile, dv), lambda b, h, p, qt, kt: (b, kt[p], h)),
            pl.BlockSpec((None, tile, dv), lambda b, h, p, qt, kt: (b, kt[p], h)),
        ],
        out_specs=pl.BlockSpec((None, tile, dv), lambda b, h, p, qt, kt: (b, qt[p], h)),
        scratch_shapes=[
            pltpu.VMEM((2, tile, 1), F32),
            pltpu.VMEM((2, tile, 1), F32),
            pltpu.VMEM((2, tile, dv), F32),
        ],
    )
    return pl.pallas_call(
        functools.partial(_attn_body, tile=tile, dk=dv // 2, lam_init=lam_init, out_scale=1.0 - lam_init),
        grid_spec=grid_spec,
        out_shape=jax.ShapeDtypeStruct((bsz, t, n_heads * dv), BF16),
        compiler_params=_cparams("parallel", "parallel", "arbitrary"),
        name="attn_prompt",
    )(qi_tab, ki_tab, slopes, lam_e, g.reshape(1, dv), q, k, v)


def _attn_sample_body(pt_ref, slope_ref, lam_ref, g_ref, q_ref, kn_ref, vn_ref, *refs,
                      n_pages, page, n_heads, dk, lam_init, out_scale):
    k_refs = refs[:n_pages]
    v_refs = refs[n_pages:2 * n_pages]
    o_ref = refs[2 * n_pages]
    dv = 2 * dk
    width = n_heads * dv
    past = n_pages * page
    nrow = 2 * n_heads

    q = q_ref[...].astype(F32)
    row = lax.broadcasted_iota(jnp.int32, (nrow, width), 0)
    lane = lax.broadcasted_iota(jnp.int32, (nrow, width), 1)
    head_of_row = row % n_heads
    map_of_row = row // n_heads
    seg = lane // dk
    sel = seg == 2 * head_of_row + map_of_row
    qm = jnp.where(sel, jnp.broadcast_to(q, (nrow, width)), 0.0)
    qm_b = qm.astype(BF16)

    r1 = lax.broadcasted_iota(jnp.int32, (nrow, 1), 0) % n_heads
    slope = jnp.zeros((nrow, 1), F32)
    for h in range(n_heads):
        slope = jnp.where(r1 == h, slope_ref[h], slope)

    pos = lax.broadcasted_iota(jnp.int32, (nrow, page), 1)
    s_pages = []
    for pg in range(n_pages):
        kp = k_refs[pg][...].astype(BF16)
        dist = (past - pg * page - pos).astype(F32)
        s_pages.append(_dot_nt(qm_b, kp) - slope * dist)
    s_new = jnp.sum(qm * kn_ref[...], axis=-1, keepdims=True)

    m = s_new
    for s in s_pages:
        m = jnp.maximum(m, jnp.max(s, axis=-1, keepdims=True))
    p_new = jnp.exp(s_new - m)
    l = p_new
    p_pages = []
    for s in s_pages:
        pp = jnp.exp(s - m)
        p_pages.append(pp)
        l = l + jnp.sum(pp, axis=-1, keepdims=True)
    lam = _lambda_value(lam_ref, lam_init)
    coef = jnp.where(r1 == lax.broadcasted_iota(jnp.int32, (nrow, 1), 0), 1.0, -lam) / l

    acc = jnp.zeros((nrow, width), F32)
    for pg in range(n_pages):
        w = (p_pages[pg] * coef).astype(BF16)
        acc = acc + _dot(w, v_refs[pg][...].astype(BF16))
    acc = acc + (p_new * coef) * vn_ref[...]
    keep = (lane // dv) == head_of_row
    per_head = jnp.where(keep, acc, 0.0)
    both = per_head[:n_heads, :] + per_head[n_heads:, :]
    ms = jnp.sum(both * both, axis=-1, keepdims=True) / dv
    normed = both * lax.rsqrt(ms + LN_EPS)
    out = jnp.sum(normed, axis=0, keepdims=True) * g_ref[...] * out_scale
    o_ref[...] = out.astype(o_ref.dtype)


def _attn_sample(q, k_new, v_new, cache_k_e, cache_v_e, page_table, slopes, lam_e, g_tiled, lam_init, n_heads, dk):
    dbs, _, width = q.shape
    n_pages = page_table.shape[1]
    page = cache_k_e.shape[1]

    def page_spec(j):
        return pl.BlockSpec((None, page, width), lambda b, pt, j=j: (pt[b, j], 0, 0))

    row_spec = pl.BlockSpec((None, 1, width), lambda b, pt: (b, 0, 0))
    grid_spec = pltpu.PrefetchScalarGridSpec(
        num_scalar_prefetch=1,
        grid=(dbs,),
        in_specs=[
            pl.BlockSpec(memory_space=pltpu.SMEM),
            pl.BlockSpec((4, lam_e.shape[1]), lambda b, pt: (0, 0)),
            pl.BlockSpec((1, width), lambda b, pt: (0, 0)),
            row_spec, row_spec, row_spec,
        ] + [page_spec(j) for j in range(n_pages)] + [page_spec(j) for j in range(n_pages)],
        out_specs=row_spec,
    )
    return pl.pallas_call(
        functools.partial(_attn_sample_body, n_pages=n_pages, page=page, n_heads=n_heads, dk=dk,
                          lam_init=lam_init, out_scale=1.0 - lam_init),
        grid_spec=grid_spec,
        out_shape=jax.ShapeDtypeStruct((dbs, 1, width), BF16),
        compiler_params=_cparams("parallel"),
        name="attn_sample",
    )(page_table, slopes, lam_e, g_tiled, q, k_new, v_new, *([cache_k_e] * n_pages), *([cache_v_e] * n_pages))


def _mlstm_gates(gates, bias):
    pre = gates + bias
    ig = GATE_CAP * jnp.tanh(pre[:, :LANES] / GATE_CAP)
    lf = _log_sigmoid(pre[:, LANES:])
    return ig, lf


def _mlstm_body(q_ref, k_ref, v_ref, o_ref, gates_ref, gbias_ref, ng_ref, h_ref, c_out, n_out, m_out,
                c_sc, n_sc, m_sc, *, n_heads, dk, dv, chunk):
    t = pl.program_id(1)

    @pl.when(t == 0)
    def _():
        c_sc[...] = jnp.zeros(c_sc.shape, F32)
        n_sc[...] = jnp.zeros(n_sc.shape, F32)
        m_sc[...] = jnp.zeros(m_sc.shape, F32)

    ig, lf = _mlstm_gates(gates_ref[...], gbias_ref[...])
    r_i = lax.broadcasted_iota(jnp.int32, (chunk, chunk), 0)
    c_i = lax.broadcasted_iota(jnp.int32, (chunk, chunk), 1)
    causal = r_i >= c_i
    tri = causal.astype(BF16)
    hi, mid, lo = _split3(lf)
    b_all = _dot(tri, hi) + _dot(tri, mid) + _dot(tri, lo)
    ig_t = ig.T
    b_t = b_all.T

    for h in range(n_heads):
        q = q_ref[:, h * dk:(h + 1) * dk]
        k = k_ref[:, h * dk:(h + 1) * dk]
        v = v_ref[:, h * dv:(h + 1) * dv]
        b_col = b_all[:, h:h + 1]
        ig_col = ig[:, h:h + 1]
        b_row = b_t[h:h + 1, :]
        ig_row = ig_t[h:h + 1, :]
        m_prev = m_sc[h]
        c_prev = c_sc[h]
        n_prev = n_sc[h]

        d = jnp.where(causal, b_col - b_row + ig_row, -jnp.inf)
        inter = b_col + m_prev
        m_t = jnp.maximum(inter, jnp.max(d, axis=-1, keepdims=True))
        w = jnp.exp(d - m_t)
        g_inter = jnp.exp(inter - m_t)
        s = _dot_nt(q, k) * w
        qf = q.astype(F32)
        num = _dot(s.astype(BF16), v) + g_inter * _dot_nt(q, c_prev.astype(BF16))
        den = jnp.sum(s, axis=-1, keepdims=True) + g_inter * jnp.sum(qf * n_prev, axis=-1, keepdims=True)
        hh = num / jnp.maximum(jnp.abs(den), jnp.exp(-m_t))
        ms = jnp.mean(hh * hh, axis=-1, keepdims=True)
        hn = hh * lax.rsqrt(ms + LN_EPS) * ng_ref[:, h * dv:(h + 1) * dv]
        h_ref[:, h * dv:(h + 1) * dv] = (hn * _sigmoid(o_ref[:, h * dv:(h + 1) * dv])).astype(h_ref.dtype)

        g_tot = b_all[chunk - 1:chunk, h:h + 1]
        dl = g_tot - b_col + ig_col
        m_new = jnp.maximum(g_tot + m_prev, jnp.max(dl, axis=0, keepdims=True))
        wl = jnp.exp(dl - m_new)
        decay = jnp.exp(g_tot + m_prev - m_new)
        kf = k.astype(F32)
        wk = wl * kf
        c_sc[h] = decay * c_prev + _dot_tn(v, wk.astype(BF16))
        n_sc[h] = decay * n_prev + jnp.sum(wk, axis=0, keepdims=True)
        m_sc[h] = m_new

    @pl.when(t == pl.num_programs(1) - 1)
    def _():
        c_out[...] = c_sc[...]
        n_out[...] = n_sc[...]
        m_out[...] = m_sc[...]


def _mlstm_prompt(q, k, v, o, gates, gbias, ng, n_heads, dk, dv, chunk):
    bsz, t, _ = q.shape
    assert t % chunk == 0
    row = lambda n: pl.BlockSpec((None, chunk, n), lambda b, i: (b, i, 0))
    const = lambda shape: pl.BlockSpec(shape, lambda b, i: (0,) * len(shape))
    outs = pl.pallas_call(
        functools.partial(_mlstm_body, n_heads=n_heads, dk=dk, dv=dv, chunk=chunk),
        grid=(bsz, t // chunk),
        in_specs=[row(n_heads * dk), row(n_heads * dk), row(n_heads * dv), row(n_heads * dv), row(2 * LANES),
                  const((1, 2 * LANES)), const((1, n_heads * dv))],
        out_specs=[
            row(n_heads * dv),
            pl.BlockSpec((None, n_heads, dv, dk), lambda b, i: (b, 0, 0, 0)),
            pl.BlockSpec((None, n_heads, 1, dk), lambda b, i: (b, 0, 0, 0)),
            pl.BlockSpec((None, n_heads, 1, 1), lambda b, i: (b, 0, 0, 0)),
        ],
        out_shape=[
            jax.ShapeDtypeStruct((bsz, t, n_heads * dv), BF16),
            jax.ShapeDtypeStruct((bsz, n_heads, dv, dk), F32),
            jax.ShapeDtypeStruct((bsz, n_heads, 1, dk), F32),
            jax.ShapeDtypeStruct((bsz, n_heads, 1, 1), F32),
        ],
        scratch_shapes=[
            pltpu.VMEM((n_heads, dv, dk), F32),
            pltpu.VMEM((n_heads, 1, dk), F32),
            pltpu.VMEM((n_heads, 1, 1), F32),
        ],
        compiler_params=_cparams("parallel", "arbitrary"),
        name="mlstm_prompt",
    )(q, k, v, o, gates, gbias, ng)
    h, c_n, n_n, m_n = outs
    return h, c_n, n_n.reshape(bsz, n_heads, dk), m_n.reshape(bsz, n_heads)


def _mlstm_sample_body(q_ref, k_ref, v_ref, o_ref, gates_ref, gbias_ref, ng_ref, m_ref, n_ref, c_ref,
                       h_ref, c_out, n_out, m_out, *, n_heads, dk, dv, group):
    g0 = pl.multiple_of(pl.program_id(0) * group, group)
    rows = pl.ds(g0, group)
    dbs = q_ref.shape[0]

    ig_a, lf_a = _mlstm_gates(gates_ref[...], gbias_ref[...])
    m_prev_a = m_ref[...]
    m_t_a = jnp.maximum(lf_a + m_prev_a, ig_a)
    wl_a = jnp.exp(ig_a - m_t_a)

    ig, lf = _mlstm_gates(gates_ref[rows, :], gbias_ref[...])
    m_prev = m_ref[rows, :]
    m_t = jnp.maximum(lf + m_prev, ig)
    w_in = jnp.exp(ig - m_t)
    decay = jnp.exp(lf + m_prev - m_t)
    m_out[...] = m_t

    lane_b = lax.broadcasted_iota(jnp.int32, (dv, dbs), 1)
    sub = lax.broadcasted_iota(jnp.int32, (group, dv), 0)
    for h in range(n_heads):
        qh = q_ref[rows, h * dk:(h + 1) * dk]
        kh = k_ref[rows, h * dk:(h + 1) * dk].astype(F32)
        vh = v_ref[rows, h * dv:(h + 1) * dv].astype(F32)
        qf = qh.astype(F32)
        n_prev = n_ref[rows, h * dk:(h + 1) * dk]
        dec_h = decay[:, h:h + 1]
        win_h = w_in[:, h:h + 1]
        mt_h = m_t[:, h:h + 1]

        v_t = v_ref[:, h * dv:(h + 1) * dv].astype(F32).T
        wk_all = (wl_a[:, h:h + 1] * k_ref[:, h * dk:(h + 1) * dk].astype(F32)).astype(BF16)

        qc = jnp.zeros((group, dv), F32)
        for j in range(group):
            c_prev = c_ref[j, h]
            r = _dot_nt(qh, c_prev.astype(BF16))
            qc = jnp.where(sub == j, r, qc)
            lhs = jnp.where(lane_b == g0 + j, v_t, 0.0).astype(BF16)
            c_out[j, h] = dec_h[j:j + 1, :] * c_prev + _dot(lhs, wk_all)

        s = jnp.sum(qf * kh, axis=-1, keepdims=True) * win_h
        num = s * vh + dec_h * qc
        den = s + dec_h * jnp.sum(qf * n_prev, axis=-1, keepdims=True)
        hh = num / jnp.maximum(jnp.abs(den), jnp.exp(-mt_h))
        ms = jnp.mean(hh * hh, axis=-1, keepdims=True)
        hn = hh * lax.rsqrt(ms + LN_EPS) * ng_ref[:, h * dv:(h + 1) * dv]
        h_ref[:, h * dv:(h + 1) * dv] = (hn * _sigmoid(o_ref[rows, h * dv:(h + 1) * dv])).astype(h_ref.dtype)
        n_out[:, h * dk:(h + 1) * dk] = dec_h * n_prev + win_h * kh


def _mlstm_sample(q, k, v, o, gates, gbias, ng, m_pad, n_prev, c_prev, n_heads, dk, dv, group):
    dbs = q.shape[0]
    assert dbs % group == 0
    full = lambda a: pl.BlockSpec(a.shape, lambda i: (0,) * a.ndim)
    grp = lambda n: pl.BlockSpec((group, n), lambda i: (i, 0))
    c_spec = pl.BlockSpec((group, n_heads, dv, dk), lambda i: (i, 0, 0, 0))
    return pl.pallas_call(
        functools.partial(_mlstm_sample_body, n_heads=n_heads, dk=dk, dv=dv, group=group),
        grid=(dbs // group,),
        in_specs=[full(q), full(k), full(v), full(o), full(gates), full(gbias), full(ng), full(m_pad), full(n_prev), c_spec],
        out_specs=[grp(n_heads * dv), c_spec, grp(n_heads * dk), grp(LANES)],
        out_shape=[
            jax.ShapeDtypeStruct((dbs, n_heads * dv), BF16),
            jax.ShapeDtypeStruct(c_prev.shape, F32),
            jax.ShapeDtypeStruct((dbs, n_heads * dk), F32),
            jax.ShapeDtypeStruct((dbs, LANES), F32),
        ],
        compiler_params=_cparams("parallel"),
        name="mlstm_sample",
    )(q, k, v, o, gates, gbias, ng, m_pad, n_prev, c_prev)


def _lru_coeffs(xc, wax_ref, ba_ref, bx_ref, lam_ref, n_blk, bw):
    r_parts, i_parts = [], []
    for n in range(n_blk):
        y = _dot(xc[:, n * bw:(n + 1) * bw].astype(BF16), wax_ref[n])
        r_parts.append(y[:, :bw])
        i_parts.append(y[:, bw:])
    r = _sigmoid(jnp.concatenate(r_parts, axis=-1) + ba_ref[...])
    i = _sigmoid(jnp.concatenate(i_parts, axis=-1) + bx_ref[...])
    log_a = -LRU_C * r * _softplus(-lam_ref[...])
    a = jnp.exp(log_a)
    b = jnp.sqrt(-jnp.tanh(log_a) * (a * a + 1.0)) * (i * xc)
    return a, b


def _rglru_body(xr_ref, g_ref, cw_ref, cb_ref, wax_ref, ba_ref, bx_ref, lam_ref, y_ref, hl_ref,
                ext_sc, a_sc, b_sc, h_sc, *, tm, n_blk, bw, conv_w):
    t = pl.program_id(1)
    pad = SUBLANES

    @pl.when(t == 0)
    def _():
        ext_sc[0:pad, :] = jnp.zeros((pad, ext_sc.shape[1]), F32)
        h_sc[...] = jnp.zeros(h_sc.shape, F32)

    ext_sc[pad:pad + tm, :] = xr_ref[...]
    xc = cb_ref[...] + cw_ref[conv_w - 1:conv_w, :] * xr_ref[...]
    for j in range(conv_w - 1):
        off = pad - (conv_w - 1) + j
        xc = xc + cw_ref[j:j + 1, :] * ext_sc[off:off + tm, :]
    ext_sc[0:pad, :] = ext_sc[tm:tm + pad, :]

    a, b = _lru_coeffs(xc, wax_ref, ba_ref, bx_ref, lam_ref, n_blk, bw)
    a_sc[...] = a
    b_sc[...] = b

    def step(i, h):
        h = a_sc[pl.ds(i, 1), :] * h + b_sc[pl.ds(i, 1), :]
        b_sc[pl.ds(i, 1), :] = h
        return h

    h_last = lax.fori_loop(0, tm, step, h_sc[...])
    h_sc[...] = h_last
    y_ref[...] = (b_sc[...] * jax.nn.gelu(g_ref[...])).astype(y_ref.dtype)

    @pl.when(t == pl.num_programs(1) - 1)
    def _():
        hl_ref[...] = h_last


def _rglru_prompt(xr, g, conv_w, conv_b, wax, b_a, b_x, lam, tm):
    bsz, t, dr = xr.shape
    n_blk, bw, _ = wax.shape
    cw = conv_w.shape[0]
    row = pl.BlockSpec((None, tm, dr), lambda b, i: (b, i, 0))
    const = lambda shape: pl.BlockSpec(shape, lambda b, i: (0,) * len(shape))
    y, h_last = pl.pallas_call(
        functools.partial(_rglru_body, tm=tm, n_blk=n_blk, bw=bw, conv_w=cw),
        grid=(bsz, t // tm),
        in_specs=[row, row, const((cw, dr)), const((1, dr)), const(wax.shape), const((1, dr)), const((1, dr)),
                  const((1, dr))],
        out_specs=[row, pl.BlockSpec((None, 1, dr), lambda b, i: (b, 0, 0))],
        out_shape=[jax.ShapeDtypeStruct((bsz, t, dr), BF16), jax.ShapeDtypeStruct((bsz, 1, dr), F32)],
        scratch_shapes=[
            pltpu.VMEM((tm + SUBLANES, dr), F32),
            pltpu.VMEM((tm, dr), F32),
            pltpu.VMEM((tm, dr), F32),
            pltpu.VMEM((1, dr), F32),
        ],
        compiler_params=_cparams("parallel", "arbitrary"),
        name="rglru_prompt",
    )(xr, g, conv_w, conv_b.reshape(1, dr), wax, b_a.reshape(1, dr), b_x.reshape(1, dr), lam.reshape(1, dr))
    return y, h_last.reshape(bsz, dr)


def _rglru_sample_body(xr_ref, g_ref, c0_ref, h0_ref, cw_ref, cb_ref, wax_ref, ba_ref, bx_ref, lam_ref,
                       y_ref, h_ref, *, n_blk, bw, conv_w):
    xc = cb_ref[...] + cw_ref[conv_w - 1:conv_w, :] * xr_ref[...]
    for j in range(conv_w - 1):
        xc = xc + cw_ref[j:j + 1, :] * c0_ref[j]
    a, b = _lru_coeffs(xc, wax_ref, ba_ref, bx_ref, lam_ref, n_blk, bw)
    h = a * h0_ref[...] + b
    h_ref[...] = h
    y_ref[...] = (h * jax.nn.gelu(g_ref[...])).astype(y_ref.dtype)


def _rglru_sample(xr, g, conv0_t, h0, conv_w, conv_b, wax, b_a, b_x, lam):
    dbs, dr = xr.shape
    n_blk, bw, _ = wax.shape
    cw = conv_w.shape[0]
    args = (xr, g, conv0_t, h0, conv_w, conv_b.reshape(1, dr), wax, b_a.reshape(1, dr), b_x.reshape(1, dr),
            lam.reshape(1, dr))
    return pl.pallas_call(
        functools.partial(_rglru_sample_body, n_blk=n_blk, bw=bw, conv_w=cw),
        grid=(1,),
        in_specs=[pl.BlockSpec(a.shape, lambda i, nd=a.ndim: (0,) * nd) for a in args],
        out_specs=[pl.BlockSpec((dbs, dr), lambda i: (0, 0)), pl.BlockSpec((dbs, dr), lambda i: (0, 0))],
        out_shape=[jax.ShapeDtypeStruct((dbs, dr), BF16), jax.ShapeDtypeStruct((dbs, dr), F32)],
        compiler_params=_cparams("arbitrary"),
        name="rglru_sample",
    )(*args)


def kernel(x_prompt, x_sample, cache_k, cache_v, state_C, state_n, state_m, state_h, state_conv, page_table,
           c_prompt, c_sample, w_ada, b_ada, ln_g, ln_b, w_in_even, w_out_even, lam, attn_norm_g, mlstm_norm_g,
           b_i, b_f, w_in_odd, w_out_odd, conv_w, conv_b, w_a, b_a, w_x, b_x, lru_lambda, w_ffn_in, w_ffn_out):
    bp, seq, d = x_prompt.shape
    dbs, dec_seq, _ = x_sample.shape
    assert dec_seq == 1, "the sample group is one new token per row"
    depth = w_ada.shape[0]
    n_even, n_pool, page, h_a, _, dk_a = cache_k.shape
    dv_a = cache_v.shape[-1]
    _, _, h_b, dv_b, dk_b = state_C.shape
    n_odd, _, d_rnn = state_h.shape
    n_blk, bw = w_a.shape[1], w_a.shape[2]
    cw = conv_w.shape[1]
    d_ff = w_ffn_out.shape[1]
    wa = h_a * dv_a
    assert 2 * dk_a == dv_a == LANES and dv_b == LANES and h_a <= LANES and h_b <= LANES
    alpha = (2.0 * depth) ** 0.25
    slopes = jnp.asarray(np.array([2.0 ** (-8.0 * (h + 1) / h_a) for h in range(h_a)], np.float32))

    r_all = -(-(dbs + bp) // SUBLANES) * SUBLANES
    c_all = jnp.concatenate([c_sample, c_prompt, jnp.zeros((r_all - dbs - bp, d), F32)], axis=0)
    mod = _modulations(c_all, w_ada.astype(BF16), b_ada)
    mod_s = mod[:, :dbs].reshape(depth, 1, dbs, 6 * d)
    mod_p = mod[:, dbs:dbs + bp].reshape(depth, bp, 1, 6 * d)

    xp = x_prompt
    xs = x_sample.reshape(1, dbs, d)

    tm_p = 512
    tm_s = dbs
    sizes = [wa, wa, wa, h_b * dk_b, h_b * dk_b, h_b * dv_b, h_b * dv_b, h_b, h_b]
    offs = np.concatenate([[0], np.cumsum(sizes)]).tolist()

    outs = {k: [] for k in ("k_p", "v_p", "k_s", "v_s", "C_p", "n_p", "m_p", "C_s", "n_s", "m_s",
                            "h_p", "conv_p", "h_s", "conv_s")}

    for li in range(depth):
        if li % 2 == 0:
            e = li // 2
            lam_init = 0.8 - 0.6 * math.exp(-0.3 * li)
            w_in = w_in_even[e]
            seg = lambda i: w_in[:, offs[i]:offs[i + 1]].astype(BF16)
            w_gates = jnp.zeros((d, 2 * LANES), F32)
            w_gates = w_gates.at[:, :h_b].set(w_in[:, offs[7]:offs[8]]).at[:, LANES:LANES + h_b].set(w_in[:, offs[8]:offs[9]])
            weights = [seg(0), seg(1), seg(2), seg(3), seg(4), seg(5), seg(6), w_gates.astype(BF16)]
            plan = (
                ((dk_a ** -0.5, BF16),),
                ((1.0, F32), (1.0, BF16)),
                ((1.0, F32), (1.0, BF16)),
                ((dk_b ** -0.5, BF16),),
                ((1.0, BF16),),
                ((1.0, BF16),),
                ((1.0, F32),),
                ((1.0, F32),),
            )
            gbias = jnp.zeros((1, 2 * LANES), F32).at[0, :h_b].set(b_i[e]).at[0, LANES:LANES + h_b].set(b_f[e])
            ng = mlstm_norm_g[e].reshape(1, h_b * dv_b)
            w_out = w_out_even[e].astype(BF16)
            w_out_a, w_out_b = w_out[:wa], w_out[wa:]

            aq, ak, akb, av, avb, mq, mk, mv, mo, mg = _inproj(xp, mod_p, li, 0, weights, plan, tm_p, "inproj_even_prompt")
            a_out = _attn_prompt(aq, akb, avb, slopes, lam[e], attn_norm_g[e], lam_init, h_a, 512)
            h_out, c_n, n_n, m_n = _mlstm_prompt(mq, mk, mv, mo, mg, gbias, ng, h_b, dk_b, dv_b, 128)
            outs["k_p"].append(ak.reshape(bp, seq // page, page, h_a, 2, dk_a))
            outs["v_p"].append(av.reshape(bp, seq // page, page, h_a, dv_a))
            outs["C_p"].append(c_n); outs["n_p"].append(n_n); outs["m_p"].append(m_n)
            xp = _outproj_ln([a_out, h_out], [w_out_a, w_out_b], xp, mod_p, li, 2, ln_g[li, 0], ln_b[li, 0], alpha,
                             tm_p, "outproj_even_prompt")

            aq, ak, akb, av, avb, mq, mk, mv, mo, mg = _inproj(xs, mod_s, li, 0, weights, plan, tm_s, "inproj_even_sample")
            r3 = lambda a: a.reshape(dbs, 1, a.shape[-1])
            a_out = _attn_sample(r3(aq), r3(ak), r3(av), cache_k[e].reshape(n_pool, page, wa),
                                 cache_v[e].reshape(n_pool, page, wa), page_table, slopes, lam[e],
                                 jnp.tile(attn_norm_g[e], h_a).reshape(1, wa), lam_init, h_a, dk_a)
            m_pad = jnp.zeros((dbs, LANES), F32).at[:, :h_b].set(state_m[e])
            r2 = lambda a: a.reshape(dbs, a.shape[-1])
            h_out, c_n, n_n, m_n = _mlstm_sample(r2(mq), r2(mk), r2(mv), r2(mo), r2(mg), gbias, ng, m_pad,
                                                 state_n[e].reshape(dbs, h_b * dk_b), state_C[e], h_b, dk_b, dv_b, 8)
            outs["k_s"].append(ak.reshape(dbs, 1, h_a, 2, dk_a))
            outs["v_s"].append(av.reshape(dbs, 1, h_a, dv_a))
            outs["C_s"].append(c_n)
            outs["n_s"].append(n_n.reshape(dbs, h_b, dk_b))
            outs["m_s"].append(m_n[:, :h_b])
            xs = _outproj_ln([a_out.reshape(1, dbs, wa), h_out.reshape(1, dbs, h_b * dv_b)], [w_out_a, w_out_b], xs,
                             mod_s, li, 2, ln_g[li, 0], ln_b[li, 0], alpha, tm_s, "outproj_even_sample")
        else:
            o = li // 2
            w_in = w_in_odd[o].astype(BF16)
            weights = [w_in[:, :d_rnn], w_in[:, d_rnn:]]
            plan = (((1.0, F32),), ((1.0, F32),))
            wax = jnp.concatenate([w_a[o], w_x[o]], axis=-1).astype(BF16)
            w_out = w_out_odd[o].astype(BF16)

            g, xr = _inproj(xp, mod_p, li, 0, weights, plan, tm_p, "inproj_odd_prompt")
            y, h_n = _rglru_prompt(xr, g, conv_w[o], conv_b[o], wax, b_a[o], b_x[o], lru_lambda[o], 256)
            outs["h_p"].append(h_n)
            outs["conv_p"].append(xr[:, seq - (cw - 1):, :])
            xp = _outproj_ln([y], [w_out], xp, mod_p, li, 2, ln_g[li, 0], ln_b[li, 0], alpha, tm_p, "outproj_odd_prompt")

            g, xr = _inproj(xs, mod_s, li, 0, weights, plan, tm_s, "inproj_odd_sample")
            conv0 = state_conv[o]
            y, h_n = _rglru_sample(xr.reshape(dbs, d_rnn), g.reshape(dbs, d_rnn), jnp.swapaxes(conv0, 0, 1), state_h[o],
                                   conv_w[o], conv_b[o], wax, b_a[o], b_x[o], lru_lambda[o])
            outs["h_s"].append(h_n)
            outs["conv_s"].append(jnp.concatenate([conv0[:, 1:], xr.reshape(dbs, 1, d_rnn)], axis=1))
            xs = _outproj_ln([y.reshape(1, dbs, d_rnn)], [w_out], xs, mod_s, li, 2, ln_g[li, 0], ln_b[li, 0], alpha,
                             tm_s, "outproj_odd_sample")

        w_fi = w_ffn_in[li].astype(BF16)
        w_fg, w_fu = w_fi[:, :d_ff], w_fi[:, d_ff:]
        w_fo = w_ffn_out[li].astype(BF16)
        tf = d_ff // 2
        xp = _ffn_ln(xp, mod_p, li, w_fg, w_fu, w_fo, ln_g[li, 1], ln_b[li, 1], alpha, tm_p, tf, "ffn_prompt")
        xs = _ffn_ln(xs, mod_s, li, w_fg, w_fu, w_fo, ln_g[li, 1], ln_b[li, 1], alpha, tm_s, tf, "ffn_sample")

    st = lambda key: jnp.stack(outs[key])
    return (xp, xs.reshape(dbs, 1, d), st("k_p"), st("v_p"), st("k_s"), st("v_s"), st("C_p"), st("n_p"), st("m_p"),
            st("C_s"), st("n_s"), st("m_s"), st("h_p"), st("conv_p"), st("h_s"), st("conv_s"))
```
